```python
import math
import jax, jax.numpy as jnp
from jax import lax
import numpy as np

D_MODEL = 1024
BATCH = 2
SEQ = 8192
DEPTH = 1
DEC_BATCH = 16
DEC_SEQ = 2048
PAST_LEN = 128

GRID_W = 64
ATT_HEADS = 8
ATT_HEAD_DIM = 64
D_ATT = ATT_HEADS * ATT_HEAD_DIM
WIN_ROWS = 8
WIN_COLS = 16
SSM_HEADS = 8
SSM_HEAD_DIM = 64
D_SSM = SSM_HEADS * SSM_HEAD_DIM
SSM_GROUPS = 2
D_STATE = 128
D_CONV = 5
CHUNK = 128
D_MIX = D_ATT + D_SSM
CONV_CH = D_SSM + 2 * SSM_GROUPS * D_STATE
D_IN_PROJ = 3 * D_ATT + D_SSM + CONV_CH + 2 * SSM_HEADS
D_FF = ((8 * D_MODEL + 3 * 256 - 1) // (3 * 256)) * 256
EPS = 1e-6

kernel_name = 'hymba_natten_ssd_encoder'


def rmsnorm(x, w):
    xf = x.astype(jnp.float32)
    y = xf * lax.rsqrt(jnp.mean(xf * xf, axis=-1, keepdims=True) + EPS)
    return (y * w.astype(jnp.float32)).astype(x.dtype)


def neighbourhood_attention(q, k, v, rpb):
    bsz, T, H, hd = q.shape
    rows = T // GRID_W
    wr = min(WIN_ROWS, rows)
    qg = q.reshape(bsz, rows, GRID_W, H, hd)
    kg = k.reshape(bsz, rows, GRID_W, H, hd)
    vg = v.reshape(bsz, rows, GRID_W, H, hd)
    cols = np.arange(GRID_W)
    c0 = np.clip(cols - WIN_COLS // 2, 0, GRID_W - WIN_COLS)
    col_idx = c0[:, None] + np.arange(WIN_COLS)[None, :]
    dc_idx = col_idx - cols[:, None] + (WIN_COLS - 1)
    rpb_c = rpb[:, :, dc_idx]

    def row_block(r):
        r0 = jnp.clip(r - WIN_ROWS // 2, 0, rows - wr)
        k_rows = lax.dynamic_slice_in_dim(kg, r0, wr, axis=1)
        v_rows = lax.dynamic_slice_in_dim(vg, r0, wr, axis=1)
        k_win = k_rows[:, :, col_idx]
        v_win = v_rows[:, :, col_idx]
        q_r = lax.dynamic_index_in_dim(qg, r, axis=1, keepdims=False)
        s = jnp.einsum('bqhd,brqchd->bhqrc', q_r, k_win).astype(jnp.float32)
        dr_idx = r0 - r + jnp.arange(wr) + (WIN_ROWS - 1)
        bias = jnp.take(rpb_c, dr_idx, axis=1).astype(jnp.float32)
        s = s + bias.transpose(0, 2, 1, 3)[None]
        p = jax.nn.softmax(s.reshape(bsz, H, GRID_W, wr * WIN_COLS), axis=-1)
        p = p.reshape(bsz, H, GRID_W, wr, WIN_COLS).astype(v.dtype)
        return jnp.einsum('bhqrc,brqchd->bqhd', p, v_win)

    out = lax.map(row_block, jnp.arange(rows))
    return out.transpose(1, 0, 2, 3, 4).reshape(bsz, T, H * hd)


def depthwise_conv(x, w, b):
    y = lax.conv_general_dilated(
        x, w[:, None, :], window_strides=(1,),
        padding=[(D_CONV // 2, D_CONV // 2)],
        dimension_numbers=('NWC', 'WIO', 'NWC'),
        feature_group_count=x.shape[-1])
    return y + b


def ssd_scan(x, dt, A, Bh, Ch):
    bsz, T, H, P = x.shape
    N = Bh.shape[-1]
    nc = T // CHUNK
    xdt = (x.astype(jnp.float32) * dt[..., None]).reshape(bsz, nc, CHUNK, H, P)
    Bc = Bh.astype(jnp.float32).reshape(bsz, nc, CHUNK, H, N)
    Cc = Ch.astype(jnp.float32).reshape(bsz, nc, CHUNK, H, N)
    a = (dt * A).reshape(bsz, nc, CHUNK, H).transpose(0, 3, 1, 2)
    a_cs = jnp.cumsum(a, axis=-1)
    tril = np.tril(np.ones((CHUNK, CHUNK), dtype=bool))
    seg = a_cs[..., :, None] - a_cs[..., None, :]
    Lmat = jnp.exp(jnp.where(tril, seg, -jnp.inf))
    cb = jnp.einsum('bclhn,bcshn->bhcls', Cc, Bc)
    y_diag = jnp.einsum('bhcls,bcshp->bclhp', cb * Lmat, xdt)
    decay_states = jnp.exp(a_cs[..., -1:] - a_cs)
    states = jnp.einsum('bclhn,bhcl,bclhp->bchpn', Bc, decay_states, xdt)
    chunk_decay = jnp.exp(a_cs[..., -1])

    def step(h, inp):
        s_c, d_c = inp
        return h * d_c[..., None, None] + s_c, h

    h0 = jnp.zeros((bsz, H, P, N), jnp.float32)
    _, prev = lax.scan(step, h0, (states.transpose(1, 0, 2, 3, 4), chunk_decay.transpose(2, 0, 1)))
    prev = prev.transpose(1, 0, 2, 3, 4)
    y_off = jnp.einsum('bclhn,bchpn,bhcl->bclhp', Cc, prev, jnp.exp(a_cs))
    return (y_diag + y_off).reshape(bsz, T, H, P)


def ssd_mixer(z, xBC, dt_f_raw, dt_b_raw, conv_w, conv_b, dt_bias_f, dt_bias_b,
              A_log_f, A_log_b, D_skip, ssm_norm_w):
    bsz, T, _ = z.shape
    xBC = jax.nn.silu(depthwise_conv(xBC, conv_w, conv_b))
    xs, Bm, Cm = jnp.split(xBC, [D_SSM, D_SSM + SSM_GROUPS * D_STATE], axis=-1)
    xs = xs.reshape(bsz, T, SSM_HEADS, SSM_HEAD_DIM)
    rep = SSM_HEADS // SSM_GROUPS
    Bh = jnp.repeat(Bm.reshape(bsz, T, SSM_GROUPS, D_STATE), rep, axis=2)
    Ch = jnp.repeat(Cm.reshape(bsz, T, SSM_GROUPS, D_STATE), rep, axis=2)
    dt_f = jax.nn.softplus(dt_f_raw.astype(jnp.float32) + dt_bias_f.astype(jnp.float32))
    dt_b = jax.nn.softplus(dt_b_raw.astype(jnp.float32) + dt_bias_b.astype(jnp.float32))
    A_f = -jnp.exp(A_log_f.astype(jnp.float32))
    A_b = -jnp.exp(A_log_b.astype(jnp.float32))
    y_f = ssd_scan(xs, dt_f, A_f, Bh, Ch)
    y_b = jnp.flip(ssd_scan(jnp.flip(xs, 1), jnp.flip(dt_b, 1), A_b,
                            jnp.flip(Bh, 1), jnp.flip(Ch, 1)), 1)
    y = y_f + y_b + D_skip.astype(jnp.float32)[:, None] * xs.astype(jnp.float32)
    g = y.reshape(bsz, T, D_SSM) * jax.nn.silu(z.astype(jnp.float32))
    g = g.reshape(bsz, T, SSM_GROUPS, D_SSM // SSM_GROUPS)
    g = g * lax.rsqrt(jnp.mean(g * g, axis=-1, keepdims=True) + EPS)
    g = g.reshape(bsz, T, D_SSM) * ssm_norm_w.astype(jnp.float32)
    return g.astype(z.dtype)


def hybrid_layer(x, norm1_w, w_in, q_norm_w, k_norm_w, rpb, conv_w, conv_b,
                 dt_bias_f, dt_bias_b, A_log_f, A_log_b, D_skip, ssm_norm_w,
                 w_out, norm2_w, w_gate, w_up, w_down):
    bsz, T, _ = x.shape
    h = rmsnorm(x, norm1_w)
    proj = h @ w_in
    splits = [D_ATT, 2 * D_ATT, 3 * D_ATT, 3 * D_ATT + D_SSM,
              3 * D_ATT + D_SSM + CONV_CH, 3 * D_ATT + D_SSM + CONV_CH + SSM_HEADS]
    q, k, v, z, xBC, dt_f, dt_b = jnp.split(proj, splits, axis=-1)
    q = rmsnorm(q.reshape(bsz, T, ATT_HEADS, ATT_HEAD_DIM), q_norm_w) * (ATT_HEAD_DIM ** -0.5)
    k = rmsnorm(k.reshape(bsz, T, ATT_HEADS, ATT_HEAD_DIM), k_norm_w)
    v = v.reshape(bsz, T, ATT_HEADS, ATT_HEAD_DIM)
    att = neighbourhood_attention(q, k, v, rpb)
    ssm = ssd_mixer(z, xBC, dt_f, dt_b, conv_w, conv_b, dt_bias_f, dt_bias_b,
                    A_log_f, A_log_b, D_skip, ssm_norm_w)
    x = x + jnp.concatenate([att, ssm], axis=-1) @ w_out
    h2 = rmsnorm(x, norm2_w)
    return x + (jax.nn.silu(h2 @ w_gate) * (h2 @ w_up)) @ w_down


def trunk(x, norm1_w, w_in, q_norm_w, k_norm_w, rpb, conv_w, conv_b,
          dt_bias_f, dt_bias_b, A_log_f, A_log_b, D_skip, ssm_norm_w,
          w_out, norm2_w, w_gate, w_up, w_down):
    for l in range(DEPTH):
        x = hybrid_layer(x, norm1_w[l], w_in[l], q_norm_w[l], k_norm_w[l], rpb[l],
                         conv_w[l], conv_b[l], dt_bias_f[l], dt_bias_b[l],
                         A_log_f[l], A_log_b[l], D_skip[l], ssm_norm_w[l],
                         w_out[l], norm2_w[l], w_gate[l], w_up[l], w_down[l])
    return x


def setup_inputs(seed: int = 0) -> dict:
    key = jax.random.key(seed)
    ks = jax.random.split(key, 24)
    f32 = jnp.float32
    L = DEPTH

    def nrm(k, shape, scale):
        return jax.random.normal(k, shape, f32) * scale

    def gain(k, shape):
        return 1.0 + 0.05 * jax.random.normal(k, shape, f32)

    def dt_bias(k):
        u = jax.random.uniform(k, (L, SSM_HEADS), f32)
        dt = jnp.exp(u * (math.log(0.1) - math.log(0.001)) + math.log(0.001))
        return dt + jnp.log(-jnp.expm1(-dt))

    return {
        'x_prompt': jax.random.normal(ks[0], (BATCH, SEQ, D_MODEL), f32),
        'x_sample': jax.random.normal(ks[1], (DEC_BATCH, DEC_SEQ, D_MODEL), f32),
        'norm1_w': gain(ks[2], (L, D_MODEL)),
        'w_in': nrm(ks[3], (L, D_MODEL, D_IN_PROJ), D_MODEL ** -0.5),
        'q_norm_w': gain(ks[4], (L, ATT_HEAD_DIM)),
        'k_norm_w': gain(ks[5], (L, ATT_HEAD_DIM)),
        'rpb': nrm(ks[6], (L, ATT_HEADS, 2 * WIN_ROWS - 1, 2 * WIN_COLS - 1), 0.1),
        'conv_w': nrm(ks[7], (L, D_CONV, CONV_CH), D_CONV ** -0.5),
        'conv_b': nrm(ks[8], (L, CONV_CH), 0.01),
        'dt_bias_f': dt_bias(ks[9]),
        'dt_bias_b': dt_bias(ks[10]),
        'A_log_f': jnp.log(jax.random.uniform(ks[11], (L, SSM_HEADS), f32, 1.0, 16.0)),
        'A_log_b': jnp.log(jax.random.uniform(ks[12], (L, SSM_HEADS), f32, 1.0, 16.0)),
        'D_skip': gain(ks[13], (L, SSM_HEADS)),
        'ssm_norm_w': gain(ks[14], (L, D_SSM)),
        'w_out': nrm(ks[15], (L, D_MIX, D_MODEL), D_MIX ** -0.5),
        'norm2_w': gain(ks[16], (L, D_MODEL)),
        'w_gate': nrm(ks[17], (L, D_MODEL, D_FF), D_MODEL ** -0.5),
        'w_up': nrm(ks[18], (L, D_MODEL, D_FF), D_MODEL ** -0.5),
        'w_down': nrm(ks[19], (L, D_FF, D_MODEL), D_FF ** -0.5),
    }


def reference(x_prompt, x_sample, norm1_w, w_in, q_norm_w, k_norm_w, rpb, conv_w, conv_b,
              dt_bias_f, dt_bias_b, A_log_f, A_log_b, D_skip, ssm_norm_w,
              w_out, norm2_w, w_gate, w_up, w_down):
    y_prompt = trunk(x_prompt, norm1_w, w_in, q_norm_w, k_norm_w, rpb, conv_w, conv_b,
                     dt_bias_f, dt_bias_b, A_log_f, A_log_b, D_skip, ssm_norm_w,
                     w_out, norm2_w, w_gate, w_up, w_down)
    y_sample = trunk(x_sample, norm1_w, w_in, q_norm_w, k_norm_w, rpb, conv_w, conv_b,
                     dt_bias_f, dt_bias_b, A_log_f, A_log_b, D_skip, ssm_norm_w,
                     w_out, norm2_w, w_gate, w_up, w_down)
    return (y_prompt, y_sample)
```

```python
import functools

import numpy as np
import jax
import jax.numpy as jnp
from jax import lax
from jax.experimental import pallas as pl
from jax.experimental.pallas import tpu as pltpu

D_MODEL = 1024
GRID_W = 64
ATT_HEADS = 8
HEAD_DIM = 64
D_ATT = ATT_HEADS * HEAD_DIM
WIN_ROWS = 8
WIN_COLS = 16
SSM_HEADS = 8
D_SSM = SSM_HEADS * HEAD_DIM
SSM_GROUPS = 2
D_STATE = 128
D_CONV = 5
CHUNK = 128
CONV_CH = D_SSM + 2 * SSM_GROUPS * D_STATE
D_MAIN = 3 * D_ATT + D_SSM + CONV_CH
EPS = 1e-6

LANES = 128
SUBLANES = 8
VMEM_REQUEST_BYTES = 56 * 2**20
MASKED = -1e30

TOKEN_TILE = 512
ATT_ROWS = 4
ATT_BLOCK = ATT_ROWS * GRID_W
SSD_CHUNKS_PER_STEP = 2

F32 = jnp.float32
BF16 = jnp.bfloat16


def _dot(a, b):
    return jnp.dot(a, b, preferred_element_type=F32)


def _dot_nt(a, b):
    return lax.dot_general(a, b, (((1,), (1,)), ((), ())), preferred_element_type=F32)


def _dot_tn(a, b):
    return lax.dot_general(a, b, (((0,), (0,)), ((), ())), preferred_element_type=F32)


def _silu(x):
    return x * jax.nn.sigmoid(x)


def _params(semantics):
    return pltpu.CompilerParams(dimension_semantics=semantics, vmem_limit_bytes=VMEM_REQUEST_BYTES)


def _const_spec(shape):
    return pl.BlockSpec(shape, lambda *_: (0,) * len(shape), pipeline_mode=pl.Buffered(1))


def _inproj_kernel(x_ref, n1_ref, w_ref, wdt_ref, qg_ref, kg_ref, gm_ref,
                   q_ref, k_ref, v_ref, z_ref, xbc_ref, dt_ref):
    x = x_ref[...]
    ms = jnp.mean(x * x, axis=-1, keepdims=True)
    h = (x * lax.rsqrt(ms + EPS) * n1_ref[...]).astype(BF16)

    def head_norm(t, g_ref):
        tt = (t * t).astype(BF16)
        gm = gm_ref[...]
        half = 2 * LANES
        msq = jnp.concatenate([_dot(tt[:, :half], gm), _dot(tt[:, half:], gm)], axis=-1)
        return (t * lax.rsqrt(msq + EPS) * g_ref[...]).astype(BF16)

    q_ref[...] = head_norm(_dot(h, w_ref[:, 0:D_ATT]), qg_ref)
    k_ref[...] = head_norm(_dot(h, w_ref[:, D_ATT:2 * D_ATT]), kg_ref)
    v_ref[...] = _dot(h, w_ref[:, 2 * D_ATT:3 * D_ATT]).astype(BF16)
    z_ref[...] = _dot(h, w_ref[:, 3 * D_ATT:3 * D_ATT + D_SSM])
    xbc_ref[...] = _dot(h, w_ref[:, 3 * D_ATT + D_SSM:D_MAIN])
    dt_ref[...] = _dot(h, wdt_ref[...])


def _inproj(x2, n1, w_main, w_dt, qg, kg, gm):
    n = x2.shape[0]
    tm = TOKEN_TILE
    row = lambda w: pl.BlockSpec((tm, w), lambda i: (i, 0))
    return pl.pallas_call(
        _inproj_kernel,
        grid=(n // tm,),
        in_specs=[row(D_MODEL), _const_spec((1, D_MODEL)), _const_spec((D_MODEL, D_MAIN)),
                  _const_spec((D_MODEL, LANES)), _const_spec((1, D_ATT)), _const_spec((1, D_ATT)),
                  _const_spec((2 * LANES, 2 * LANES))],
        out_specs=[row(D_ATT), row(D_ATT), row(D_ATT), row(D_SSM), row(CONV_CH), row(LANES)],
        out_shape=[jax.ShapeDtypeStruct((n, D_ATT), BF16)] * 3
        + [jax.ShapeDtypeStruct((n, D_SSM), F32), jax.ShapeDtypeStruct((n, CONV_CH), F32),
           jax.ShapeDtypeStruct((n, LANES), F32)],
        compiler_params=_params(("parallel",)),
        name="inproj",
    )(x2, n1, w_main, w_dt, qg, kg, gm)


def _attn_kernel(q_ref, kp_ref, kc_ref, kn_ref, vp_ref, vc_ref, vn_ref, mb_ref, o_ref):
    k_refs = (kp_ref, kc_ref, kn_ref)
    v_refs = (vp_ref, vc_ref, vn_ref)
    outs = []
    for h in range(ATT_HEADS):
        sl = slice(h * HEAD_DIM, (h + 1) * HEAD_DIM)
        qh = q_ref[0, :, sl]
        s = [_dot_nt(qh, k_refs[j][0, :, sl]) + mb_ref[0, h, :, j * ATT_BLOCK:(j + 1) * ATT_BLOCK]
             for j in range(3)]
        m = jnp.max(jnp.maximum(jnp.maximum(s[0], s[1]), s[2]), axis=-1, keepdims=True)
        p = [jnp.exp(sj - m) for sj in s]
        denom = jnp.sum(p[0] + p[1] + p[2], axis=-1, keepdims=True)
        o = (_dot(p[0].astype(BF16), v_refs[0][0, :, sl])
             + _dot(p[1].astype(BF16), v_refs[1][0, :, sl])
             + _dot(p[2].astype(BF16), v_refs[2][0, :, sl]))
        outs.append(o / denom)
    o_ref[0] = jnp.concatenate(outs, axis=-1).astype(BF16)


def _attention_bias_table(rpb, rows):
    assert rows >= 4 * ATT_ROWS
    slab_rows = 3 * ATT_ROWS
    i = np.arange(ATT_ROWS)[:, None, None, None]
    qc = np.arange(GRID_W)[None, :, None, None]
    j = np.arange(slab_rows)[None, None, :, None]
    kc = np.arange(GRID_W)[None, None, None, :]
    c0 = np.clip(qc - WIN_COLS // 2, 0, GRID_W - WIN_COLS)
    col_ok = (kc >= c0) & (kc < c0 + WIN_COLS)
    dc = np.clip(kc - qc + (WIN_COLS - 1), 0, 2 * WIN_COLS - 2)
    dr = np.clip(j - i + (WIN_ROWS - 1) - ATT_ROWS, 0, 2 * WIN_ROWS - 2)
    row_ok = np.stack([
        (j >= ATT_ROWS) & (j < ATT_ROWS + WIN_ROWS) & (i >= 0),
        (j >= i) & (j < i + WIN_ROWS),
        (j < WIN_ROWS) & (i >= 0),
    ])
    shape = (ATT_ROWS, GRID_W, slab_rows, GRID_W)
    ok = (row_ok & col_ok[None]).reshape(3, ATT_BLOCK, 3 * ATT_BLOCK)
    dr = np.broadcast_to(dr, shape).reshape(ATT_BLOCK, 3 * ATT_BLOCK)
    dc = np.broadcast_to(dc, shape).reshape(ATT_BLOCK, 3 * ATT_BLOCK)
    bias = rpb.astype(F32)[:, dr, dc]
    return jnp.where(ok[:, None], bias[None], MASKED)


def _attention(q, k, v, table):
    b, t, _ = q.shape
    nblk = t // ATT_BLOCK
    blk = lambda f: pl.BlockSpec((1, ATT_BLOCK, D_ATT), f)
    cur = lambda i, bb: (bb, i, 0)
    prev = lambda i, bb: (bb, jnp.maximum(i - 1, 0), 0)
    nxt = lambda i, bb: (bb, jnp.minimum(i + 1, nblk - 1), 0)
    variant = lambda i, bb: (jnp.where(i == 0, 0, jnp.where(i == nblk - 1, 2, 1)), 0, 0, 0)
    return pl.pallas_call(
        _attn_kernel,
        grid=(nblk, b),
        in_specs=[blk(cur), blk(prev), blk(cur), blk(nxt), blk(prev), blk(cur), blk(nxt),
                  pl.BlockSpec((1, ATT_HEADS, ATT_BLOCK, 3 * ATT_BLOCK), variant)],
        out_specs=blk(cur),
        out_shape=jax.ShapeDtypeStruct((b, t, D_ATT), BF16),
        compiler_params=_params(("arbitrary", "arbitrary")),
        name="nattn",
    )(q, k, k, k, v, v, v, table)


def _softplus(x):
    return jnp.maximum(x, 0.0) + jnp.log1p(jnp.exp(-jnp.abs(x)))


def _ssd_chunk(dt_raw_rows, dt_bias_col, a_log_col, xs, bm, cm, st_ref, reverse):
    L = CHUNK
    lane8 = lax.broadcasted_iota(jnp.int32, (SUBLANES, L), 1)
    dt = _softplus(dt_raw_rows + dt_bias_col)
    acs = dt * (-jnp.exp(a_log_col))
    shift = 1
    while shift < L:
        if reverse:
            acs = acs + jnp.where(lane8 < L - shift, pltpu.roll(acs, L - shift, axis=1), 0.0)
        else:
            acs = acs + jnp.where(lane8 >= shift, pltpu.roll(acs, shift, axis=1), 0.0)
        shift *= 2
    tot = acs[:, 0:1] if reverse else acs[:, L - 1:L]
    tot = jnp.broadcast_to(tot, (SUBLANES, L))
    w_rows = jnp.exp(tot - acs) * dt
    e_rows = jnp.exp(acs)
    chunk_decay = jnp.exp(tot)
    stacked = jnp.concatenate(
        [acs, w_rows, e_rows, jnp.zeros((L - 3 * SUBLANES, L), F32)], axis=0)
    cols = stacked.T

    row_id = lax.broadcasted_iota(jnp.int32, (L, L), 0)
    col_id = lax.broadcasted_iota(jnp.int32, (L, L), 1)
    causal = (col_id >= row_id) if reverse else (col_id <= row_id)
    low_half = col_id < HEAD_DIM

    def bcast_col(c):
        return jnp.broadcast_to(cols[:, c:c + 1], (L, L))

    ys = []
    heads_per_group = SSM_HEADS // SSM_GROUPS
    cb = None
    for pair in range(SSM_HEADS // 2):
        g = (2 * pair) // heads_per_group
        if (2 * pair) % heads_per_group == 0:
            b_g = bm[:, g * D_STATE:(g + 1) * D_STATE]
            c_g = cm[:, g * D_STATE:(g + 1) * D_STATE]
            cb = _dot_nt(c_g.astype(BF16), b_g.astype(BF16))
        m_parts, ce_parts, bw_parts = [], [], []
        for h in (2 * pair, 2 * pair + 1):
            seg = bcast_col(h) - acs[h:h + 1, :]
            decay = jnp.exp(jnp.where(causal, seg, MASKED))
            m_parts.append((cb * decay * dt[h:h + 1, :]).astype(BF16))
            bw_parts.append((b_g * bcast_col(SUBLANES + h)).astype(BF16))
            ce_parts.append((c_g * bcast_col(2 * SUBLANES + h)).astype(BF16))
        psl = slice(pair * LANES, (pair + 1) * LANES)
        x_pair = xs[:, psl]
        x_lo = jnp.where(low_half, x_pair, 0.0).astype(BF16)
        x_hi = jnp.where(low_half, 0.0, x_pair).astype(BF16)
        s_pair = st_ref[:, psl]
        s_lo = jnp.where(low_half, s_pair, 0.0).astype(BF16)
        s_hi = jnp.where(low_half, 0.0, s_pair).astype(BF16)
        lhs = jnp.concatenate(m_parts + ce_parts, axis=1)
        rhs = jnp.concatenate([x_lo, x_hi, s_lo, s_hi], axis=0)
        ys.append(_dot(lhs, rhs))
        new_state = _dot_tn(jnp.concatenate(bw_parts, axis=0), jnp.concatenate([x_lo, x_hi], axis=0))
        scale = jnp.where(low_half[0:1, :], chunk_decay[2 * pair:2 * pair + 1, :],
                          chunk_decay[2 * pair + 1:2 * pair + 2, :])
        st_ref[:, psl] = s_pair * scale + new_state
    return jnp.concatenate(ys, axis=1)


def _ssd_fwd_kernel(xp_ref, xc_ref, xn_ref, dt_ref, cw_ref, cbias_ref, dtb_ref, alog_ref, dskip_ref,
                    yf_ref, xb_ref, bc_ref, xe_ref, st_ref):
    step = pl.program_id(1)
    rows = SSD_CHUNKS_PER_STEP * CHUNK

    @pl.when(step == 0)
    def _():
        st_ref[...] = jnp.zeros_like(st_ref)

    halo = SUBLANES
    xe_ref[0:halo, :] = jnp.where(step > 0, xp_ref[0], 0.0)
    xe_ref[halo:halo + rows, :] = xc_ref[0]
    xe_ref[halo + rows:2 * halo + rows, :] = jnp.where(step < pl.num_programs(1) - 1, xn_ref[0], 0.0)

    for ci in range(SSD_CHUNKS_PER_STEP):
        base = ci * CHUNK
        conv = jnp.broadcast_to(cbias_ref[...], (CHUNK, CONV_CH))
        for tap in range(D_CONV):
            start = base + halo - D_CONV // 2 + tap
            conv = conv + cw_ref[tap:tap + 1, :] * xe_ref[start:start + CHUNK, :]
        u = _silu(conv)
        xs = u[:, :D_SSM]
        bm = u[:, D_SSM:D_SSM + SSM_GROUPS * D_STATE]
        cm = u[:, D_SSM + SSM_GROUPS * D_STATE:]
        dt_rows = dt_ref[0, base:base + CHUNK, :].T[0:SSM_HEADS, :]
        y = _ssd_chunk(dt_rows, dtb_ref[...], alog_ref[...], xs, bm, cm, st_ref, reverse=False)
        yf_ref[0, base:base + CHUNK, :] = y + dskip_ref[...] * xs
        xb_ref[0, base:base + CHUNK, :] = xs.astype(BF16)
        bc_ref[0, base:base + CHUNK, :] = u[:, D_SSM:].astype(BF16)


def _ssd_bwd_kernel(yf_ref, xb_ref, bc_ref, z_ref, dt_ref, dtb_ref, alog_ref, nw_ref,
                    o_ref, st_ref):
    step = pl.program_id(1)

    @pl.when(step == 0)
    def _():
        st_ref[...] = jnp.zeros_like(st_ref)

    for ci in reversed(range(SSD_CHUNKS_PER_STEP)):
        base = ci * CHUNK
        xs = xb_ref[0, base:base + CHUNK, :].astype(F32)
        bc = bc_ref[0, base:base + CHUNK, :].astype(F32)
        bm = bc[:, :SSM_GROUPS * D_STATE]
        cm = bc[:, SSM_GROUPS * D_STATE:]
        dt_rows = dt_ref[0, base:base + CHUNK, :].T[SSM_HEADS:2 * SSM_HEADS, :]
        y = _ssd_chunk(dt_rows, dtb_ref[...], alog_ref[...], xs, bm, cm, st_ref, reverse=True)
        y = y + yf_ref[0, base:base + CHUNK, :]
        gated = y * _silu(z_ref[0, base:base + CHUNK, :])
        gw = D_SSM // SSM_GROUPS
        normed = []
        for g in range(SSM_GROUPS):
            gg = gated[:, g * gw:(g + 1) * gw]
            normed.append(gg * lax.rsqrt(jnp.mean(gg * gg, axis=-1, keepdims=True) + EPS))
        o_ref[0, base:base + CHUNK, :] = (jnp.concatenate(normed, axis=-1) * nw_ref[...]).astype(BF16)


def _ssd_fwd(xbc, dt, conv_w, conv_b, dt_bias, a_log, d_skip):
    b, t, _ = xbc.shape
    rows = SSD_CHUNKS_PER_STEP * CHUNK
    steps = t // rows
    per_halo = rows // SUBLANES
    n_halo = t // SUBLANES
    cur = lambda bb, s: (bb, s, 0)
    blk = lambda w: pl.BlockSpec((1, rows, w), cur)
    halo_prev = pl.BlockSpec((1, SUBLANES, CONV_CH),
                             lambda bb, s: (bb, jnp.maximum(s * per_halo - 1, 0), 0))
    halo_next = pl.BlockSpec((1, SUBLANES, CONV_CH),
                             lambda bb, s: (bb, jnp.minimum((s + 1) * per_halo, n_halo - 1), 0))
    return pl.pallas_call(
        _ssd_fwd_kernel,
        grid=(b, steps),
        in_specs=[halo_prev, blk(CONV_CH), halo_next, blk(LANES),
                  _const_spec((D_CONV, CONV_CH)), _const_spec((1, CONV_CH)),
                  _const_spec((SSM_HEADS, 1)), _const_spec((SSM_HEADS, 1)), _const_spec((1, D_SSM))],
        out_specs=[blk(D_SSM), blk(D_SSM), blk(2 * SSM_GROUPS * D_STATE)],
        out_shape=[jax.ShapeDtypeStruct((b, t, D_SSM), F32), jax.ShapeDtypeStruct((b, t, D_SSM), BF16),
                   jax.ShapeDtypeStruct((b, t, 2 * SSM_GROUPS * D_STATE), BF16)],
        scratch_shapes=[pltpu.VMEM((rows + 2 * SUBLANES, CONV_CH), F32),
                        pltpu.VMEM((D_STATE, D_SSM), F32)],
        compiler_params=_params(("arbitrary", "arbitrary")),
        name="ssd_fwd",
    )(xbc, xbc, xbc, dt, conv_w, conv_b, dt_bias, a_log, d_skip)


def _ssd_bwd(yf, xb, bc, z, dt, dt_bias, a_log, norm_w):
    b, t, _ = yf.shape
    rows = SSD_CHUNKS_PER_STEP * CHUNK
    steps = t // rows
    rev = lambda bb, s: (bb, steps - 1 - s, 0)
    blk = lambda w: pl.BlockSpec((1, rows, w), rev)
    return pl.pallas_call(
        _ssd_bwd_kernel,
        grid=(b, steps),
        in_specs=[blk(D_SSM), blk(D_SSM), blk(2 * SSM_GROUPS * D_STATE), blk(D_SSM), blk(LANES),
                  _const_spec((SSM_HEADS, 1)), _const_spec((SSM_HEADS, 1)), _const_spec((1, D_SSM))],
        out_specs=blk(D_SSM),
        out_shape=jax.ShapeDtypeStruct((b, t, D_SSM), BF16),
        scratch_shapes=[pltpu.VMEM((D_STATE, D_SSM), F32)],
        compiler_params=_params(("arbitrary", "arbitrary")),
        name="ssd_bwd",
    )(yf, xb, bc, z, dt, dt_bias, a_log, norm_w)


def _ffn_kernel(x_ref, att_ref, ssm_ref, wo_ref, n2_ref, wg_ref, wu_ref, wd_ref, o_ref):
    x1 = (x_ref[...] + _dot(att_ref[...], wo_ref[0:D_ATT, :])
          + _dot(ssm_ref[...], wo_ref[D_ATT:D_ATT + D_SSM, :]))
    ms = jnp.mean(x1 * x1, axis=-1, keepdims=True)
    h2 = (x1 * lax.rsqrt(ms + EPS) * n2_ref[...]).astype(BF16)
    act = (_silu(_dot(h2, wg_ref[...])) * _dot(h2, wu_ref[...])).astype(BF16)
    o_ref[...] = x1 + _dot(act, wd_ref[...])


def _ffn(x2, att, ssm, w_out, n2, w_gate, w_up, w_down):
    n = x2.shape[0]
    d_ff = w_gate.shape[1]
    tm = TOKEN_TILE
    row = lambda w: pl.BlockSpec((tm, w), lambda i: (i, 0))
    return pl.pallas_call(
        _ffn_kernel,
        grid=(n // tm,),
        in_specs=[row(D_MODEL), row(D_ATT), row(D_SSM), _const_spec((D_ATT + D_SSM, D_MODEL)),
                  _const_spec((1, D_MODEL)), _const_spec((D_MODEL, d_ff)), _const_spec((D_MODEL, d_ff)),
                  _const_spec((d_ff, D_MODEL))],
        out_specs=row(D_MODEL),
        out_shape=jax.ShapeDtypeStruct((n, D_MODEL), F32),
        compiler_params=_params(("parallel",)),
        name="outproj_ffn",
    )(x2, att, ssm, w_out, n2, w_gate, w_up, w_down)


def _prepare_layer(norm1_w, w_in, q_norm_w, k_norm_w, rpb, conv_w, conv_b, dt_bias_f, dt_bias_b,
                   A_log_f, A_log_b, D_skip, ssm_norm_w, w_out, norm2_w, w_gate, w_up, w_down):
    w_dt = jnp.pad(w_in[:, D_MAIN:], ((0, 0), (0, LANES - 2 * SSM_HEADS))).astype(BF16)
    head_avg = np.kron(np.eye(2 * LANES // HEAD_DIM), np.full((HEAD_DIM, HEAD_DIM), 1.0 / HEAD_DIM))
    col = lambda p: p.astype(F32).reshape(-1, 1)
    row = lambda p: p.astype(F32).reshape(1, -1)
    return dict(
        n1=row(norm1_w), w_main=w_in[:, :D_MAIN].astype(BF16), w_dt=w_dt,
        qg=row(jnp.tile(q_norm_w, ATT_HEADS)) * (HEAD_DIM ** -0.5), kg=row(jnp.tile(k_norm_w, ATT_HEADS)),
        gm=jnp.asarray(head_avg, BF16), rpb=rpb,
        conv_w=conv_w.astype(F32), conv_b=row(conv_b),
        dtb_f=col(dt_bias_f), dtb_b=col(dt_bias_b), alog_f=col(A_log_f), alog_b=col(A_log_b),
        d_skip=row(jnp.repeat(D_skip, HEAD_DIM)), ssm_nw=row(ssm_norm_w),
        w_out=w_out.astype(BF16), n2=row(norm2_w),
        w_gate=w_gate.astype(BF16), w_up=w_up.astype(BF16), w_down=w_down.astype(BF16))


def _layer(x, p):
    b, t, _ = x.shape
    n = b * t
    x2 = x.reshape(n, D_MODEL)
    q, k, v, z, xbc, dt = _inproj(x2, p["n1"], p["w_main"], p["w_dt"], p["qg"], p["kg"], p["gm"])
    seq = lambda a: a.reshape(b, t, a.shape[-1])
    table = _attention_bias_table(p["rpb"], t // GRID_W)
    att = _attention(seq(q), seq(k), seq(v), table)
    yf, xb, bc = _ssd_fwd(seq(xbc), seq(dt), p["conv_w"], p["conv_b"], p["dtb_f"], p["alog_f"], p["d_skip"])
    ssm = _ssd_bwd(yf, xb, bc, seq(z), seq(dt), p["dtb_b"], p["alog_b"], p["ssm_nw"])
    y = _ffn(x2, att.reshape(n, D_ATT), ssm.reshape(n, D_SSM), p["w_out"], p["n2"],
             p["w_gate"], p["w_up"], p["w_down"])
    return y.reshape(b, t, D_MODEL)


def kernel(x_prompt, x_sample, norm1_w, w_in, q_norm_w, k_norm_w, rpb, conv_w, conv_b, dt_bias_f,
           dt_bias_b, A_log_f, A_log_b, D_skip, ssm_norm_w, w_out, norm2_w, w_gate, w_up, w_down):
    stacked = (norm1_w, w_in, q_norm_w, k_norm_w, rpb, conv_w, conv_b, dt_bias_f, dt_bias_b,
               A_log_f, A_log_b, D_skip, ssm_norm_w, w_out, norm2_w, w_gate, w_up, w_down)
    y_prompt, y_sample = x_prompt, x_sample
    for layer in range(norm1_w.shape[0]):
        p = _prepare_layer(*(w[layer] for w in stacked))
        y_prompt = _layer(y_prompt, p)
        y_sample = _layer(y_sample, p)
    return (y_prompt, y_sample)
```
